```python
import jax
import jax.numpy as jnp
from jax import lax
import numpy as np

D_MODEL = 2048
BATCH = 2
SEQ = 8192
DEPTH = 4

GRID_W = 64
CTX_LEN = 256
EPS = 1e-6
LOG_FLOOR = 1e-30
N_EVEN = (DEPTH + 1) // 2
N_ODD = DEPTH // 2
N_MOD = 6

A_HEADS = 8
A_DK = 128
A_DV = D_MODEL // 16
A_KWIDTH = A_HEADS * A_DK
A_WIDTH = A_HEADS * A_DV
SCAN_CHUNK = 64

B_GROUPS = 8
B_CH = D_MODEL // 16
B_WIDTH = B_GROUPS * B_CH
B_CHUNK = 128

Q_OFF = 0
FF_OFF = Q_OFF + A_KWIDTH
FB_OFF = FF_OFF + A_KWIDTH
I_OFF = FB_OFF + A_KWIDTH
G_OFF = I_OFF + A_WIDTH
U_OFF = G_OFF + A_WIDTH
V_OFF = U_OFF + B_WIDTH
IN_COLS = V_OFF + B_WIDTH
MIX_WIDTH = A_WIDTH + B_WIDTH

C_GROUPS = 4
C_CH = D_MODEL // C_GROUPS
POOL_WINDOWS = (2, 4, 8, 16)

D_FF = 4 * D_MODEL

kernel_name = 'hybrid_hgrn2_chunkmlp_pool_dit'


def rms_norm(x, g):
    xf = x.astype(jnp.float32)
    y = xf * lax.rsqrt(jnp.mean(xf * xf, axis=-1, keepdims=True) + EPS)
    return (y * g.astype(jnp.float32)).astype(x.dtype)


def modulate(h, shift, scale):
    return h * (1.0 + scale[:, None, :]) + shift[:, None, :]


def ada_mods(cvec, w, b):
    m = jax.nn.silu(cvec) @ w + b
    return jnp.split(m, N_MOD, axis=-1)


def split_heads(t, dh):
    return t.reshape(t.shape[:-1] + (t.shape[-1] // dh, dh))


def flip_seq(t):
    return jnp.flip(t, axis=1)


def lower_bounds(lb_logits):
    p = jax.nn.softmax(lb_logits.astype(jnp.float32), axis=1)
    return jnp.cumsum(p, axis=1) - p[:, :1]


def hgrn_gates(f_pre, lb):
    f = f_pre.astype(jnp.float32)
    fg = lb + (1.0 - lb) * jax.nn.sigmoid(f)
    log_f = jnp.log(jnp.maximum(fg, LOG_FLOOR))
    k = (1.0 - lb) * jax.nn.sigmoid(-f)
    return split_heads(k, A_DK), split_heads(log_f, A_DK)


def gla_chunk_scan(q, k, v, log_f, s0):
    bsz, seq_len, n_heads, _ = q.shape
    dv = v.shape[-1]
    n_chunks = seq_len // SCAN_CHUNK

    def to_chunks(t):
        return t.reshape(bsz, n_chunks, SCAN_CHUNK, n_heads, t.shape[-1]).transpose(1, 0, 3, 2, 4)

    lower = jnp.tril(jnp.ones((SCAN_CHUNK, SCAN_CHUNK), dtype=bool))[None, None, :, :, None]

    def step(state, blk):
        qb, kb, vb, gb = blk
        b = jnp.cumsum(gb, axis=2)
        o_inter = jnp.einsum('bhtk,bhkv->bhtv', qb * jnp.exp(b), state)
        diff = b[:, :, :, None, :] - b[:, :, None, :, :]
        decay = jnp.where(lower, jnp.exp(jnp.minimum(diff, 0.0)), 0.0)
        scores = jnp.einsum('bhtk,bhsk,bhtsk->bhts', qb, kb, decay)
        o_intra = jnp.einsum('bhts,bhsv->bhtv', scores, vb)
        b_last = b[:, :, -1:, :]
        new_state = (jnp.exp(b_last[:, :, 0, :])[..., None] * state
                     + jnp.einsum('bhsk,bhsv->bhkv', kb * jnp.exp(b_last - b), vb))
        return new_state, o_inter + o_intra

    s_fin, o = lax.scan(step, s0, (to_chunks(q), to_chunks(k), to_chunks(v), to_chunks(log_f)))
    o = o.transpose(1, 0, 3, 2, 4).reshape(bsz, seq_len, n_heads, dv)
    return o, s_fin


def gla_final_state(k, v, log_f):
    b = jnp.cumsum(log_f, axis=1)
    w = jnp.exp(b[:, -1:] - b)
    return jnp.einsum('blhk,blhv->bhkv', k * w, v)


def hgrn_stream(q, f_fwd, f_bwd, i, lb_f, lb_b, s0_f, s0_b):
    kf, gf = hgrn_gates(f_fwd, lb_f)
    kb, gb = hgrn_gates(f_bwd, lb_b)
    qh = split_heads(q.astype(jnp.float32), A_DK)
    vh = split_heads(i.astype(jnp.float32), A_DV)
    o_f, s_f = gla_chunk_scan(qh, kf, vh, gf, s0_f)
    o_b, s_b = gla_chunk_scan(flip_seq(qh), flip_seq(kb), flip_seq(vh), flip_seq(gb), s0_b)
    return o_f + flip_seq(o_b), s_f, s_b


def hgrn_final_states(f_fwd, f_bwd, i, lb_f, lb_b):
    kf, gf = hgrn_gates(f_fwd, lb_f)
    kb, gb = hgrn_gates(f_bwd, lb_b)
    vh = split_heads(i.astype(jnp.float32), A_DV)
    s_f = gla_final_state(kf, vh, gf)
    s_b = gla_final_state(flip_seq(kb), flip_seq(vh), flip_seq(gb))
    return s_f, s_b


def hgrn_readout(o, g, norm_g):
    o = o * lax.rsqrt(jnp.mean(o * o, axis=-1, keepdims=True) + EPS)
    o = o.reshape(o.shape[:2] + (A_WIDTH,)) * norm_g.astype(jnp.float32)
    return (o * jax.nn.silu(g.astype(jnp.float32))).astype(g.dtype)


def chunk_token_mlp(u, v, ws, bias, vnorm_g):
    bsz, seq_len, _ = u.shape
    u32 = jax.nn.gelu(u.astype(jnp.float32), approximate=False)
    vg = split_heads(jax.nn.gelu(v.astype(jnp.float32), approximate=False), B_CH)
    mu = jnp.mean(vg, axis=-1, keepdims=True)
    var = jnp.mean(jnp.square(vg - mu), axis=-1, keepdims=True)
    vn = (vg - mu) * lax.rsqrt(var + EPS) * split_heads(vnorm_g.astype(jnp.float32), B_CH)
    vc = vn.reshape(bsz, seq_len // B_CHUNK, B_CHUNK, B_GROUPS, B_CH)
    mixed = (jnp.einsum('gts,bnsgc->bntgc', ws.astype(jnp.float32), vc)
             + bias.astype(jnp.float32).T[:, :, None])
    return (u32 * mixed.reshape(bsz, seq_len, B_WIDTH)).astype(v.dtype)


def even_mixer(h_lat, h_ctx, w_in, w_out, lb_f, lb_b, a_norm_g, b_ws, b_bias, b_vnorm_g, ctx_out_needed):
    bsz = h_ctx.shape[0]
    if ctx_out_needed:
        zero = jnp.zeros((bsz, A_HEADS, A_DK, A_DV), jnp.float32)
        pc = h_ctx @ w_in
        oc, s_f, s_b = hgrn_stream(pc[..., Q_OFF:FF_OFF], pc[..., FF_OFF:FB_OFF], pc[..., FB_OFF:I_OFF],
                                   pc[..., I_OFF:G_OFF], lb_f, lb_b, zero, zero)
        yc = jnp.concatenate([hgrn_readout(oc, pc[..., G_OFF:U_OFF], a_norm_g),
                              chunk_token_mlp(pc[..., U_OFF:V_OFF], pc[..., V_OFF:IN_COLS], b_ws, b_bias, b_vnorm_g)],
                             axis=-1) @ w_out
    else:
        pc = h_ctx @ w_in[:, FF_OFF:G_OFF]
        s_f, s_b = hgrn_final_states(pc[..., :A_KWIDTH], pc[..., A_KWIDTH:2 * A_KWIDTH],
                                     pc[..., 2 * A_KWIDTH:], lb_f, lb_b)
        yc = None
    pl = h_lat @ w_in
    ol, _, _ = hgrn_stream(pl[..., Q_OFF:FF_OFF], pl[..., FF_OFF:FB_OFF], pl[..., FB_OFF:I_OFF],
                           pl[..., I_OFF:G_OFF], lb_f, lb_b, s_f, s_b)
    yl = jnp.concatenate([hgrn_readout(ol, pl[..., G_OFF:U_OFF], a_norm_g),
                          chunk_token_mlp(pl[..., U_OFF:V_OFF], pl[..., V_OFF:IN_COLS], b_ws, b_bias, b_vnorm_g)],
                         axis=-1) @ w_out
    return yl, yc


def window_bounds(n, k):
    t = np.arange(n)
    return np.clip(t - k // 2, 0, n), np.clip(t - k // 2 + k, 0, n)


def pool_minus_self_2d(x, k):
    bsz, seq_len, ch = x.shape
    rows = seq_len // GRID_W
    g = x.reshape(bsz, rows, GRID_W, ch)
    sat = jnp.pad(jnp.cumsum(jnp.cumsum(g, axis=1), axis=2), ((0, 0), (1, 0), (1, 0), (0, 0)))
    rlo, rhi = window_bounds(rows, k)
    clo, chi = window_bounds(GRID_W, k)

    def corner(r, cidx):
        return sat[:, r][:, :, cidx]

    s = corner(rhi, chi) - corner(rlo, chi) - corner(rhi, clo) + corner(rlo, clo)
    cnt = ((rhi - rlo)[:, None] * (chi - clo)[None, :]).astype(np.float32)
    return (s / cnt[None, :, :, None] - g).reshape(bsz, seq_len, ch)


def pool_minus_self_1d(x, k):
    seq_len = x.shape[1]
    cs = jnp.pad(jnp.cumsum(x, axis=1), ((0, 0), (1, 0), (0, 0)))
    lo, hi = window_bounds(seq_len, k)
    return (cs[:, hi] - cs[:, lo]) / (hi - lo).astype(np.float32)[None, :, None] - x


def pool_mixer(h, w_in, w_grp, b_grp, scale, on_grid):
    p = (h @ w_in).astype(jnp.float32)
    pool = pool_minus_self_2d if on_grid else pool_minus_self_1d
    z = jnp.stack([pool(p[..., gi * C_CH:(gi + 1) * C_CH], k) for gi, k in enumerate(POOL_WINDOWS)], axis=2)
    y = jnp.einsum('blgc,gcd->blgd', z, w_grp.astype(jnp.float32)) + b_grp.astype(jnp.float32)
    return (y.reshape(h.shape[:2] + (D_MODEL,)) * scale.astype(jnp.float32)).astype(h.dtype)


def sq_relu_mlp(h, w1, w2):
    return jnp.square(jax.nn.relu(h @ w1)) @ w2


def setup_inputs(seed: int = 0) -> dict:
    key = jax.random.key(seed)
    ks = jax.random.split(key, 22)

    def nrm(k, shape, s):
        return jax.random.normal(k, shape, jnp.float32) * s

    d = D_MODEL
    return {
        'x': nrm(ks[0], (BATCH, SEQ, d), 1.0),
        'c': nrm(ks[1], (BATCH, d), 1.0),
        'ctx': nrm(ks[2], (BATCH, CTX_LEN, d), 1.0),
        'c_ctx': nrm(ks[3], (d,), 1.0),
        'w_ada': nrm(ks[4], (DEPTH, d, N_MOD * d), 0.5 * d ** -0.5),
        'b_ada': nrm(ks[5], (DEPTH, N_MOD * d), 0.01),
        'g_norm_mix': 1.0 + nrm(ks[6], (DEPTH, d), 0.05),
        'g_norm_ffn': 1.0 + nrm(ks[7], (DEPTH, d), 0.05),
        'w_in_even': nrm(ks[8], (N_EVEN, d, IN_COLS), d ** -0.5),
        'w_out_even': nrm(ks[9], (N_EVEN, MIX_WIDTH, d), MIX_WIDTH ** -0.5),
        'lb_logits': nrm(ks[10], (2, N_EVEN, A_KWIDTH), 0.5),
        'g_hgrn_out': 1.0 + nrm(ks[11], (N_EVEN, A_WIDTH), 0.05),
        'w_spatial': nrm(ks[12], (N_EVEN, B_GROUPS, B_CHUNK, B_CHUNK), B_CHUNK ** -0.5),
        'b_spatial': 1.0 + nrm(ks[13], (N_EVEN, B_GROUPS, B_CHUNK), 0.02),
        'g_spatial_v': 1.0 + nrm(ks[14], (N_EVEN, B_WIDTH), 0.05),
        'w_in_pool': nrm(ks[15], (N_ODD, d, d), d ** -0.5),
        'w_grp_pool': nrm(ks[16], (N_ODD, C_GROUPS, C_CH, C_CH), C_CH ** -0.5),
        'b_grp_pool': nrm(ks[17], (N_ODD, C_GROUPS, C_CH), 0.01),
        'scale_pool': 1.0 + nrm(ks[18], (N_ODD, d), 0.05),
        'w_ffn_up': nrm(ks[19], (DEPTH, d, D_FF), d ** -0.5),
        'w_ffn_down': nrm(ks[20], (DEPTH, D_FF, d), D_FF ** -0.5),
        'g_norm_final': 1.0 + nrm(ks[21], (d,), 0.05),
    }


def reference(x, c, ctx, c_ctx, w_ada, b_ada, g_norm_mix, g_norm_ffn, w_in_even, w_out_even, lb_logits,
              g_hgrn_out, w_spatial, b_spatial, g_spatial_v, w_in_pool, w_grp_pool, b_grp_pool, scale_pool,
              w_ffn_up, w_ffn_down, g_norm_final):
    lbs = lower_bounds(lb_logits)
    last_even = DEPTH - 1 if (DEPTH - 1) % 2 == 0 else DEPTH - 2
    c_ctx_row = c_ctx[None, :]
    ctx_s = ctx
    for layer in range(DEPTH):
        ctx_out_needed = layer < last_even
        ctx_read = layer <= last_even
        sh1, sc1, gt1, sh2, sc2, gt2 = ada_mods(c, w_ada[layer], b_ada[layer])
        hl = modulate(rms_norm(x, g_norm_mix[layer]), sh1, sc1)
        hc = None
        if ctx_read:
            csh1, csc1, cgt1, csh2, csc2, cgt2 = ada_mods(c_ctx_row, w_ada[layer], b_ada[layer])
            hc = modulate(rms_norm(ctx_s, g_norm_mix[layer]), csh1, csc1)
        if layer % 2 == 0:
            e = layer // 2
            yl, yc = even_mixer(hl, hc, w_in_even[e], w_out_even[e], lbs[0, e], lbs[1, e], g_hgrn_out[e],
                                w_spatial[e], b_spatial[e], g_spatial_v[e], ctx_out_needed)
        else:
            o = layer // 2
            yl = pool_mixer(hl, w_in_pool[o], w_grp_pool[o], b_grp_pool[o], scale_pool[o], True)
            yc = pool_mixer(hc, w_in_pool[o], w_grp_pool[o], b_grp_pool[o], scale_pool[o], False) if ctx_out_needed else None
        x = x + gt1[:, None, :] * yl
        x = x + gt2[:, None, :] * sq_relu_mlp(modulate(rms_norm(x, g_norm_ffn[layer]), sh2, sc2),
                                              w_ffn_up[layer], w_ffn_down[layer])
        if ctx_out_needed:
            ctx_s = ctx_s + cgt1[:, None, :] * yc
            ctx_s = ctx_s + cgt2[:, None, :] * sq_relu_mlp(modulate(rms_norm(ctx_s, g_norm_ffn[layer]), csh2, csc2),
                                                          w_ffn_up[layer], w_ffn_down[layer])
    return rms_norm(x, g_norm_final)
```

```python
import functools

import numpy as np
import jax
import jax.numpy as jnp
from jax import lax
from jax.experimental import pallas as pl
from jax.experimental.pallas import tpu as pltpu

F32 = jnp.float32
BF16 = jnp.bfloat16

EPS = 1e-6
LOG_FLOOR = 1e-30
N_MOD = 6

HEAD_DIM = 128
A_HEADS = 8
B_GROUPS = 8
A_KWIDTH = A_HEADS * HEAD_DIM
A_WIDTH = A_HEADS * HEAD_DIM
B_WIDTH = B_GROUPS * HEAD_DIM
B_CHUNK = 128
Q_BLK, FF_BLK, FB_BLK, I_BLK = 0, 8, 16, 24
G_BLK, U_BLK, V_BLK = 4, 5, 6
GRID_W = 64
POOL_WINDOWS = (2, 4, 8, 16)
C_GROUPS = 4

SCAN_C = 64
SCAN_LEVELS = 6
POOL_MM = 256
POOL_HALO = 512

VMEM_LIMIT = 56 * 1024 * 1024


def _cparams(n_grid):
    return pltpu.CompilerParams(dimension_semantics=("arbitrary",) * n_grid,
                                vmem_limit_bytes=VMEM_LIMIT)


def _dot(a, b):
    return jnp.dot(a, b, preferred_element_type=F32)


def _dot_nt(a, b):
    return lax.dot_general(a, b, (((1,), (1,)), ((), ())), preferred_element_type=F32)


def _dot_tn(a, b):
    return lax.dot_general(a, b, (((0,), (0,)), ((), ())), preferred_element_type=F32)


def _split3(x):
    hi = x.astype(BF16)
    r1 = x - hi.astype(F32)
    mid = r1.astype(BF16)
    lo = (r1 - mid.astype(F32)).astype(BF16)
    return hi, mid, lo


def _norm_mod(x, g, sh, sc):
    ms = jnp.mean(x * x, axis=-1, keepdims=True)
    return (x * lax.rsqrt(ms + EPS) * g) * (1.0 + sc) + sh


def _ada_kernel(ct_ref, w_ref, b_ref, o_ref):
    ct = ct_ref[...]
    st = ct * (1.0 / (1.0 + jnp.exp(-ct)))
    w = w_ref[...]
    b = b_ref[...]
    rows = [jnp.sum(st[:, r:r + 1] * w, axis=0, keepdims=True) + b for r in range(3)]
    rows.append(jnp.zeros((5, w.shape[1]), F32))
    o_ref[...] = jnp.concatenate(rows, axis=0)


def _ada_mods(ct, w_ada, b_ada, tn=512):
    depth, d, n = w_ada.shape
    return pl.pallas_call(
        _ada_kernel,
        grid=(depth, n // tn),
        in_specs=[pl.BlockSpec((d, 8), lambda l, j: (0, 0)),
                  pl.BlockSpec((None, d, tn), lambda l, j: (l, 0, j)),
                  pl.BlockSpec((None, 1, tn), lambda l, j: (l, 0, j))],
        out_specs=pl.BlockSpec((None, 8, tn), lambda l, j: (l, 0, j)),
        out_shape=jax.ShapeDtypeStruct((depth, 8, n), F32),
        compiler_params=_cparams(2),
        name="ada_mods",
    )(ct, w_ada, b_ada.reshape(depth, 1, n))


def _proj_kernel(x_ref, g_ref, sh_ref, sc_ref, w_ref, o_ref, h_ref):
    @pl.when(pl.program_id(2) == 0)
    def _():
        h_ref[...] = _norm_mod(x_ref[...], g_ref[...], sh_ref[...], sc_ref[...]).astype(BF16)

    o_ref[...] = _dot(h_ref[...], w_ref[...])


def _norm_mod_proj(x, g, sh, sc, w, tm=512, tn=1024):
    bsz, seq, d = x.shape
    n = w.shape[1]
    tm = min(tm, seq)
    tn = min(tn, n)
    return pl.pallas_call(
        _proj_kernel,
        grid=(bsz, seq // tm, n // tn),
        in_specs=[pl.BlockSpec((None, tm, d), lambda b, i, j: (b, i, 0)),
                  pl.BlockSpec((1, d), lambda b, i, j: (0, 0)),
                  pl.BlockSpec((None, 1, d), lambda b, i, j: (b, 0, 0)),
                  pl.BlockSpec((None, 1, d), lambda b, i, j: (b, 0, 0)),
                  pl.BlockSpec((d, tn), lambda b, i, j: (0, j))],
        out_specs=pl.BlockSpec((None, tm, tn), lambda b, i, j: (b, i, j)),
        out_shape=jax.ShapeDtypeStruct((bsz, seq, n), F32),
        scratch_shapes=[pltpu.VMEM((tm, d), BF16)],
        compiler_params=_cparams(3),
        name="norm_mod_proj",
    )(x, g, sh, sc, w)


def _ffn_kernel(x_ref, g_ref, sh_ref, sc_ref, gt_ref, w1_ref, w2_ref, gf_ref, o_ref, h_ref, acc_ref,
                *, final_norm):
    k = pl.program_id(2)

    @pl.when(k == 0)
    def _():
        h_ref[...] = _norm_mod(x_ref[...], g_ref[...], sh_ref[...], sc_ref[...]).astype(BF16)
        acc_ref[...] = jnp.zeros_like(acc_ref)

    u = jnp.maximum(_dot(h_ref[...], w1_ref[...]), 0.0)
    acc_ref[...] += _dot((u * u).astype(BF16), w2_ref[...])

    @pl.when(k == pl.num_programs(2) - 1)
    def _():
        y = x_ref[...] + gt_ref[...] * acc_ref[...]
        if final_norm:
            ms = jnp.mean(y * y, axis=-1, keepdims=True)
            y = y * lax.rsqrt(ms + EPS) * gf_ref[...]
        o_ref[...] = y


def _ffn(x, g, sh, sc, gt, w1, w2, g_final, final_norm, tm=512, tf=512):
    bsz, seq, d = x.shape
    f = w1.shape[1]
    tm = min(tm, seq)
    return pl.pallas_call(
        functools.partial(_ffn_kernel, final_norm=final_norm),
        grid=(bsz, seq // tm, f // tf),
        in_specs=[pl.BlockSpec((None, tm, d), lambda b, i, k: (b, i, 0)),
                  pl.BlockSpec((1, d), lambda b, i, k: (0, 0)),
                  pl.BlockSpec((None, 1, d), lambda b, i, k: (b, 0, 0)),
                  pl.BlockSpec((None, 1, d), lambda b, i, k: (b, 0, 0)),
                  pl.BlockSpec((None, 1, d), lambda b, i, k: (b, 0, 0)),
                  pl.BlockSpec((d, tf), lambda b, i, k: (0, k)),
                  pl.BlockSpec((tf, d), lambda b, i, k: (k, 0)),
                  pl.BlockSpec((1, d), lambda b, i, k: (0, 0))],
        out_specs=pl.BlockSpec((None, tm, d), lambda b, i, k: (b, i, 0)),
        out_shape=jax.ShapeDtypeStruct((bsz, seq, d), F32),
        scratch_shapes=[pltpu.VMEM((tm, d), BF16), pltpu.VMEM((tm, d), F32)],
        compiler_params=_cparams(3),
        name="ffn",
    )(x, g, sh, sc, gt, w1, w2, g_final)


def _scan_constants(reverse):
    c = SCAN_C
    t = np.arange(c)[:, None]
    r = np.arange(c)[None, :]
    if reverse:
        sums = [(r >= t)]
    else:
        sums = [(r <= t)]
    masks = [(r == t)]
    for lvl in range(SCAN_LEVELS):
        h = 1 << lvl
        blk = (t // (2 * h)) * (2 * h)
        same = (t // (2 * h)) == (r // (2 * h))
        if reverse:
            sums.append(r >= blk + h)
            masks.append(same & (t % (2 * h) < h) & (r % (2 * h) >= h))
        else:
            sums.append(r <= blk + h - 1)
            masks.append(same & (t % (2 * h) >= h) & (r % (2 * h) < h))
    sums = np.concatenate(sums, axis=0).astype(np.float32)
    sums3 = np.concatenate([sums, sums, sums], axis=1)
    masks = np.stack(masks, axis=0).astype(np.float32)
    return jnp.asarray(sums3, BF16), jnp.asarray(masks, F32)


def _scan_kernel(q_ref, f_ref, v_ref, lb_ref, s0_ref, sums_ref, masks_ref, o_ref, sT_ref, *, reverse):
    c = SCAN_C
    n_chunks = q_ref.shape[0] // c

    @pl.when(pl.program_id(2) == 0)
    def _():
        sT_ref[...] = s0_ref[...]

    lb = lb_ref[...]
    sums = sums_ref[...]
    order = range(n_chunks - 1, -1, -1) if reverse else range(n_chunks)
    for ci in order:
        rows = pl.ds(ci * c, c)
        q = q_ref[rows, :]
        f = f_ref[rows, :]
        v = v_ref[rows, :].astype(BF16)
        e = jnp.exp(-jnp.abs(f))
        r = 1.0 / (1.0 + e)
        er = e * r
        pos = f >= 0.0
        fg = lb + (1.0 - lb) * jnp.where(pos, r, er)
        g = jnp.log(jnp.maximum(fg, LOG_FLOOR))
        k = (1.0 - lb) * jnp.where(pos, er, r)

        allb = _dot(sums, jnp.concatenate(_split3(g), axis=0))
        b = allb[0:c]
        a = masks_ref[0] * _dot_nt(q.astype(BF16), k.astype(BF16))
        for lvl in range(SCAN_LEVELS):
            mid = allb[(lvl + 1) * c:(lvl + 2) * c]
            w = jnp.exp(-jnp.abs(b - mid))
            a = a + masks_ref[lvl + 1] * _dot_nt((q * w).astype(BF16), (k * w).astype(BF16))

        b_end = b[0:1] if reverse else b[c - 1:c]
        sT = sT_ref[...]
        o = _dot_nt((q * jnp.exp(b)).astype(BF16), sT.astype(BF16)) + _dot(a.astype(BF16), v)
        o_ref[rows, :] = o
        kd = (k * jnp.exp(b_end - b)).astype(BF16)
        sT_ref[...] = sT * jnp.exp(b_end) + _dot_tn(v, kd)


def _hgrn_scan(p, lb, s0, reverse, f_blk, tt=256):
    bsz, seq, _ = p.shape
    tt = min(tt, seq)
    nt = seq // tt
    sums, masks = _scan_constants(reverse)
    if reverse:
        tmap = lambda t: nt - 1 - t
    else:
        tmap = lambda t: t
    hd = HEAD_DIM
    return pl.pallas_call(
        functools.partial(_scan_kernel, reverse=reverse),
        grid=(bsz, A_HEADS, nt),
        in_specs=[pl.BlockSpec((None, tt, hd), lambda b, h, t: (b, tmap(t), Q_BLK + h)),
                  pl.BlockSpec((None, tt, hd), lambda b, h, t: (b, tmap(t), f_blk + h)),
                  pl.BlockSpec((None, tt, hd), lambda b, h, t: (b, tmap(t), I_BLK + h)),
                  pl.BlockSpec((1, hd), lambda b, h, t: (0, h)),
                  pl.BlockSpec((None, None, hd, hd), lambda b, h, t: (b, h, 0, 0)),
                  pl.BlockSpec(sums.shape, lambda b, h, t: (0, 0)),
                  pl.BlockSpec(masks.shape, lambda b, h, t: (0, 0, 0))],
        out_specs=[pl.BlockSpec((None, tt, hd), lambda b, h, t: (b, tmap(t), h)),
                   pl.BlockSpec((None, None, hd, hd), lambda b, h, t: (b, h, 0, 0))],
        out_shape=[jax.ShapeDtypeStruct((bsz, seq, A_WIDTH), F32),
                   jax.ShapeDtypeStruct((bsz, A_HEADS, hd, hd), F32)],
        compiler_params=_cparams(3),
        name="hgrn_scan_bwd" if reverse else "hgrn_scan_fwd",
    )(p, p, p, lb, s0, sums, masks)


def _gelu(x):
    return 0.5 * x * (1.0 + lax.erf(x * np.float32(1.0 / np.sqrt(2.0))))


def _mix_out_kernel(of_ref, ob_ref, g_ref, u_ref, v_ref, x_ref, gt_ref, an_ref, vn_ref, ws_ref, bs_ref,
                    wo_ref, o_ref, y_ref):
    tm = x_ref.shape[0]
    hd = HEAD_DIM
    o = of_ref[...] + ob_ref[...]
    gate = g_ref[...]
    gate = gate * (1.0 / (1.0 + jnp.exp(-gate)))
    an = an_ref[...]
    for h in range(A_HEADS):
        cols = slice(h * hd, (h + 1) * hd)
        oh = o[:, cols]
        ms = jnp.mean(oh * oh, axis=-1, keepdims=True)
        y_ref[:, cols] = (oh * lax.rsqrt(ms + EPS) * an[:, cols] * gate[:, cols]).astype(BF16)

    u = _gelu(u_ref[...])
    vg = _gelu(v_ref[...])
    vng = vn_ref[...]
    for gi in range(B_GROUPS):
        cols = slice(gi * hd, (gi + 1) * hd)
        vh = vg[:, cols]
        mu = jnp.mean(vh, axis=-1, keepdims=True)
        dv = vh - mu
        var = jnp.mean(dv * dv, axis=-1, keepdims=True)
        vn = (dv * lax.rsqrt(var + EPS) * vng[:, cols]).astype(BF16)
        ws = ws_ref[gi]
        bias = bs_ref[gi]
        for n in range(tm // B_CHUNK):
            rows = slice(n * B_CHUNK, (n + 1) * B_CHUNK)
            mixed = _dot(ws, vn[rows]) + bias
            y_ref[rows, A_WIDTH + gi * hd:A_WIDTH + (gi + 1) * hd] = (u[rows, cols] * mixed).astype(BF16)

    o_ref[...] = x_ref[...] + gt_ref[...] * _dot(y_ref[...], wo_ref[...])


def _mix_out(o_f, o_b, p, x, gt, a_norm_g, b_vnorm_g, ws, bias, w_out, tm=256):
    bsz, seq, d = x.shape
    tm = min(tm, seq)
    wd = A_WIDTH
    return pl.pallas_call(
        _mix_out_kernel,
        grid=(bsz, seq // tm),
        in_specs=[pl.BlockSpec((None, tm, wd), lambda b, i: (b, i, 0)),
                  pl.BlockSpec((None, tm, wd), lambda b, i: (b, i, 0)),
                  pl.BlockSpec((None, tm, wd), lambda b, i: (b, i, G_BLK)),
                  pl.BlockSpec((None, tm, wd), lambda b, i: (b, i, U_BLK)),
                  pl.BlockSpec((None, tm, wd), lambda b, i: (b, i, V_BLK)),
                  pl.BlockSpec((None, tm, d), lambda b, i: (b, i, 0)),
                  pl.BlockSpec((None, 1, d), lambda b, i: (b, 0, 0)),
                  pl.BlockSpec((1, wd), lambda b, i: (0, 0)),
                  pl.BlockSpec((1, wd), lambda b, i: (0, 0)),
                  pl.BlockSpec(ws.shape, lambda b, i: (0, 0, 0)),
                  pl.BlockSpec(bias.shape, lambda b, i: (0, 0, 0)),
                  pl.BlockSpec(w_out.shape, lambda b, i: (0, 0))],
        out_specs=pl.BlockSpec((None, tm, d), lambda b, i: (b, i, 0)),
        out_shape=jax.ShapeDtypeStruct((bsz, seq, d), F32),
        scratch_shapes=[pltpu.VMEM((tm, A_WIDTH + B_WIDTH), BF16)],
        compiler_params=_cparams(2),
        name="mix_out",
    )(o_f, o_b, p, p, p, x, gt, a_norm_g, b_vnorm_g, ws, bias, w_out)


def _pool_constants(width):
    t = np.arange(POOL_MM)[:, None]
    s = np.arange(POOL_MM)[None, :]
    mats = []
    for k in POOL_WINDOWS:
        same_row = (t // width) == (s // width)
        lo = (t % width) - k // 2
        m = same_row & ((s % width) >= lo) & ((s % width) < lo + k)
        mats.append(np.concatenate([m, m, m], axis=1))
    return jnp.asarray(np.stack(mats).astype(np.float32), BF16)


def _pool_group(k, width, n_rows, has_halo, p_ref, prev_ref, next_ref, x_ref, gt_ref, wg_ref, bg_ref, sp_ref,
                cm_ref, o_ref):
    tp = p_ref.shape[0]
    i = pl.program_id(1)
    cur = p_ref[...]
    if has_halo:
        half = (k // 2) * width
        zeros = jnp.zeros((half, cur.shape[1]), F32)
        top = jnp.where(i > 0, prev_ref[POOL_HALO - half:, :], zeros)
        bot = jnp.where(i < pl.num_programs(1) - 1, next_ref[:half, :], zeros)
        acc = jnp.concatenate([top, cur, bot], axis=0)
        step = width
        while step < k * width:
            acc = acc[:acc.shape[0] - step] + acc[step:]
            step *= 2
        rs = acc[:tp]
    else:
        rs = cur

    tok = lax.broadcasted_iota(jnp.int32, (tp, 1), 0)
    col = tok & (width - 1)
    row = (tok >> (width.bit_length() - 1)) + i * (tp // width)
    ccnt = jnp.minimum(col - k // 2 + k, width) - jnp.maximum(col - k // 2, 0)
    rcnt = jnp.minimum(row - k // 2 + k, n_rows) - jnp.maximum(row - k // 2, 0)
    inv = 1.0 / (ccnt * rcnt).astype(F32)

    cm = cm_ref[...]
    parts = []
    for n in range(tp // POOL_MM):
        blk = rs[n * POOL_MM:(n + 1) * POOL_MM]
        parts.append(_dot(cm, jnp.concatenate(_split3(blk), axis=0)))
    pooled = jnp.concatenate(parts, axis=0) if len(parts) > 1 else parts[0]
    z = pooled * inv - cur
    y = (_dot(z.astype(BF16), wg_ref[...]) + bg_ref[...]) * sp_ref[...]
    o_ref[...] = x_ref[...] + gt_ref[...] * y


def _pool_kernel(*refs, width, n_rows, has_halo):
    if has_halo:
        p_ref, prev_ref, next_ref = refs[:3]
        rest = refs[3:]
    else:
        p_ref, prev_ref, next_ref = refs[0], None, None
        rest = refs[1:]
    gi = pl.program_id(2)
    for idx, k in enumerate(POOL_WINDOWS):
        @pl.when(gi == idx)
        def _(k=k):
            _pool_group(k, width, n_rows, has_halo, p_ref, prev_ref, next_ref, *rest)


def _pool_mix(p, x, gt, w_grp, b_grp, scale, on_grid, tp=1024):
    bsz, seq, d = x.shape
    cc = d // C_GROUPS
    if on_grid:
        width, n_rows = GRID_W, seq // GRID_W
        tp = min(tp, seq)
    else:
        width, n_rows = seq, 1
        tp = seq
    has_halo = n_rows > 1
    assert POOL_MM % width == 0 and tp % POOL_MM == 0
    cm = _pool_constants(width)
    nt = seq // tp
    in_specs = [pl.BlockSpec((None, tp, cc), lambda b, i, g: (b, i, g))]
    args = [p]
    if has_halo:
        assert tp % POOL_HALO == 0 and max(POOL_WINDOWS) // 2 * width <= POOL_HALO
        ratio = tp // POOL_HALO
        last = seq // POOL_HALO - 1
        in_specs += [
            pl.BlockSpec((None, POOL_HALO, cc), lambda b, i, g: (b, jnp.maximum(i * ratio - 1, 0), g)),
            pl.BlockSpec((None, POOL_HALO, cc), lambda b, i, g: (b, jnp.minimum((i + 1) * ratio, last), g))]
        args += [p, p]
    in_specs += [pl.BlockSpec((None, tp, cc), lambda b, i, g: (b, i, g)),
                 pl.BlockSpec((None, 1, cc), lambda b, i, g: (b, 0, g)),
                 pl.BlockSpec((None, cc, cc), lambda b, i, g: (g, 0, 0)),
                 pl.BlockSpec((None, 1, cc), lambda b, i, g: (g, 0, 0)),
                 pl.BlockSpec((1, cc), lambda b, i, g: (0, g)),
                 pl.BlockSpec((None,) + cm.shape[1:], lambda b, i, g: (g, 0, 0))]
    args += [x, gt, w_grp, b_grp.reshape(C_GROUPS, 1, cc), scale, cm]
    return pl.pallas_call(
        functools.partial(_pool_kernel, width=width, n_rows=n_rows, has_halo=has_halo),
        grid=(bsz, nt, C_GROUPS),
        in_specs=in_specs,
        out_specs=pl.BlockSpec((None, tp, cc), lambda b, i, g: (b, i, g)),
        out_shape=jax.ShapeDtypeStruct((bsz, seq, d), F32),
        compiler_params=_cparams(3),
        name="pool_mix_grid" if on_grid else "pool_mix_seq",
    )(*args)


def _lower_bounds(lb_logits):
    p = jax.nn.softmax(lb_logits.astype(F32), axis=1)
    return jnp.cumsum(p, axis=1) - p[:, :1]


def kernel(x, c, ctx, c_ctx, w_ada, b_ada, g_norm_mix, g_norm_ffn, w_in_even, w_out_even, lb_logits,
           g_hgrn_out, w_spatial, b_spatial, g_spatial_v, w_in_pool, w_grp_pool, b_grp_pool, scale_pool,
           w_ffn_up, w_ffn_down, g_norm_final):
    bsz, seq, d = x.shape
    depth = w_ada.shape[0]
    last_even = depth - 1 if (depth - 1) % 2 == 0 else depth - 2
    assert bsz + 1 <= 8

    ct = jnp.zeros((d, 8), F32).at[:, :bsz].set(c.T).at[:, bsz].set(c_ctx)
    mods = _ada_mods(ct, w_ada, b_ada).reshape(depth, 8, N_MOD, d)

    def lat_mod(layer, m):
        return mods[layer, :bsz, m][:, None, :]

    def ctx_mod(layer, m):
        return jnp.broadcast_to(mods[layer, bsz, m][None, None, :], (bsz, 1, d))

    lbs = _lower_bounds(lb_logits)
    w_in_even_b = w_in_even.astype(BF16)
    w_out_even_b = w_out_even.astype(BF16)
    w_spatial_b = w_spatial.astype(BF16)
    w_in_pool_b = w_in_pool.astype(BF16)
    w_grp_pool_b = w_grp_pool.astype(BF16)
    w_up_b = w_ffn_up.astype(BF16)
    w_down_b = w_ffn_down.astype(BF16)
    g_final = g_norm_final.reshape(1, d)
    zero_state = jnp.zeros((bsz, A_HEADS, HEAD_DIM, HEAD_DIM), F32)

    def even_mixer(h_in, mod, layer, s0_f, s0_b, need_out):
        e = layer // 2
        g_mix = g_norm_mix[layer].reshape(1, d)
        p = _norm_mod_proj(h_in, g_mix, mod(layer, 0), mod(layer, 1), w_in_even_b[e])
        o_f, s_f = _hgrn_scan(p, lbs[0, e].reshape(1, -1), s0_f, False, FF_BLK)
        o_b, s_b = _hgrn_scan(p, lbs[1, e].reshape(1, -1), s0_b, True, FB_BLK)
        if not need_out:
            return None, s_f, s_b
        y = _mix_out(o_f, o_b, p, h_in, mod(layer, 2), g_hgrn_out[e].reshape(1, -1),
                     g_spatial_v[e].reshape(1, -1), w_spatial_b[e], b_spatial[e][:, :, None], w_out_even_b[e])
        return y, s_f, s_b

    def pool_mixer(h_in, mod, layer, on_grid):
        o = layer // 2
        g_mix = g_norm_mix[layer].reshape(1, d)
        p = _norm_mod_proj(h_in, g_mix, mod(layer, 0), mod(layer, 1), w_in_pool_b[o])
        return _pool_mix(p, h_in, mod(layer, 2), w_grp_pool_b[o], b_grp_pool[o], scale_pool[o].reshape(1, d),
                         on_grid)

    def ffn(h_in, mod, layer, final_norm):
        return _ffn(h_in, g_norm_ffn[layer].reshape(1, d), mod(layer, 3), mod(layer, 4), mod(layer, 5),
                    w_up_b[layer], w_down_b[layer], g_final, final_norm)

    ctx_s = ctx
    for layer in range(depth):
        ctx_out_needed = layer < last_even
        ctx_read = layer <= last_even
        if layer % 2 == 0:
            if ctx_read:
                yc, s_f, s_b = even_mixer(ctx_s, ctx_mod, layer, zero_state, zero_state, ctx_out_needed)
            else:
                yc, s_f, s_b = None, zero_state, zero_state
            x, _, _ = even_mixer(x, lat_mod, layer, s_f, s_b, True)
        else:
            yc = pool_mixer(ctx_s, ctx_mod, layer, False) if ctx_out_needed else None
            x = pool_mixer(x, lat_mod, layer, True)
        x = ffn(x, lat_mod, layer, layer == depth - 1)
        if ctx_out_needed:
            ctx_s = ffn(yc, ctx_mod, layer, False)
    return x
```

```python
import functools

import numpy as np
import jax
import jax.numpy as jnp
from jax import lax
from jax.experimental import pallas as pl
from jax.experimental.pallas import tpu as pltpu

F32 = jnp.float32
BF16 = jnp.bfloat16

EPS = 1e-6
LOG_FLOOR = 1e-30
N_MOD = 6

HEAD_DIM = 128
A_HEADS = 8
B_GROUPS = 8
A_KWIDTH = A_HEADS * HEAD_DIM
A_WIDTH = A_HEADS * HEAD_DIM
B_WIDTH = B_GROUPS * HEAD_DIM
B_CHUNK = 128
Q_BLK, FF_BLK, FB_BLK, I_BLK = 0, 8, 16, 24
G_BLK, U_BLK, V_BLK = 4, 5, 6
GRID_W = 64
POOL_WINDOWS = (2, 4, 8, 16)
C_GROUPS = 4

SCAN_C = 64
SCAN_LEVELS = 6
POOL_MM = 256
POOL_HALO = 512

VMEM_LIMIT = 56 * 1024 * 1024
LOG2E = float(np.log2(np.e))


def _cparams(n_grid):
    return pltpu.CompilerParams(dimension_semantics=("arbitrary",) * n_grid,
                                vmem_limit_bytes=VMEM_LIMIT)


def _dot(a, b):
    return jnp.dot(a, b, preferred_element_type=F32)


def _dot_nt(a, b):
    return lax.dot_general(a, b, (((1,), (1,)), ((), ())), preferred_element_type=F32)


def _dot_tn(a, b):
    return lax.dot_general(a, b, (((0,), (0,)), ((), ())), preferred_element_type=F32)


def _split3(x):
    hi = x.astype(BF16)
    r1 = x - hi.astype(F32)
    mid = r1.astype(BF16)
    lo = (r1 - mid.astype(F32)).astype(BF16)
    return hi, mid, lo


def _norm_mod(x, g, sh, sc):
    ms = jnp.mean(x * x, axis=-1, keepdims=True)
    return (x * lax.rsqrt(ms + EPS) * g) * (1.0 + sc) + sh


def _ada_kernel(ct_ref, w_ref, b_ref, o_ref):
    ct = ct_ref[...]
    st = ct * (1.0 / (1.0 + jnp.exp(-ct)))
    w = w_ref[...]
    b = b_ref[...]
    rows = [jnp.sum(st[:, r:r + 1] * w, axis=0, keepdims=True) + b for r in range(3)]
    rows.append(jnp.zeros((5, w.shape[1]), F32))
    o_ref[...] = jnp.concatenate(rows, axis=0)


def _ada_mods(ct, w_ada, b_ada, tn=512):
    depth, d, n = w_ada.shape
    return pl.pallas_call(
        _ada_kernel,
        grid=(depth, n // tn),
        in_specs=[pl.BlockSpec((d, 8), lambda l, j: (0, 0)),
                  pl.BlockSpec((None, d, tn), lambda l, j: (l, 0, j)),
                  pl.BlockSpec((None, 1, tn), lambda l, j: (l, 0, j))],
        out_specs=pl.BlockSpec((None, 8, tn), lambda l, j: (l, 0, j)),
        out_shape=jax.ShapeDtypeStruct((depth, 8, n), F32),
        compiler_params=_cparams(2),
        name="ada_mods",
    )(ct, w_ada, b_ada.reshape(depth, 1, n))


def _proj_kernel(x_ref, g_ref, sh_ref, sc_ref, w_ref, o_ref, h_ref):
    @pl.when(pl.program_id(2) == 0)
    def _():
        h_ref[...] = _norm_mod(x_ref[...], g_ref[...], sh_ref[...], sc_ref[...]).astype(BF16)

    o_ref[...] = _dot(h_ref[...], w_ref[...])


def _norm_mod_proj(x, g, sh, sc, w, layer, tm=1024, tn=1024):
    bsz, seq, d = x.shape
    n = w.shape[2]
    tm = min(tm, seq)
    tn = min(tn, n)
    return pl.pallas_call(
        _proj_kernel,
        grid=(bsz, seq // tm, n // tn),
        in_specs=[pl.BlockSpec((None, tm, d), lambda b, i, j: (b, i, 0)),
                  pl.BlockSpec((1, d), lambda b, i, j: (0, 0)),
                  pl.BlockSpec((None, 1, d), lambda b, i, j: (b, 0, 0)),
                  pl.BlockSpec((None, 1, d), lambda b, i, j: (b, 0, 0)),
                  pl.BlockSpec((None, d, tn), lambda b, i, j: (layer, 0, j))],
        out_specs=pl.BlockSpec((None, tm, tn), lambda b, i, j: (b, i, j)),
        out_shape=jax.ShapeDtypeStruct((bsz, seq, n), F32),
        scratch_shapes=[pltpu.VMEM((tm, d), BF16)],
        compiler_params=_cparams(3),
        name="norm_mod_proj",
    )(x, g, sh, sc, w)


def _ffn_kernel(x_ref, g_ref, sh_ref, sc_ref, gt_ref, w1_ref, w2_ref, gf_ref, o_ref, h_ref, *, final_norm):
    k = pl.program_id(2)

    @pl.when(k == 0)
    def _():
        h_ref[...] = _norm_mod(x_ref[...], g_ref[...], sh_ref[...], sc_ref[...]).astype(BF16)
        o_ref[...] = jnp.zeros_like(o_ref)

    u = jnp.maximum(_dot(h_ref[...], w1_ref[...]), 0.0)
    o_ref[...] += _dot((u * u).astype(BF16), w2_ref[...])

    @pl.when(k == pl.num_programs(2) - 1)
    def _():
        y = x_ref[...] + gt_ref[...] * o_ref[...]
        if final_norm:
            ms = jnp.mean(y * y, axis=-1, keepdims=True)
            y = y * lax.rsqrt(ms + EPS) * gf_ref[...]
        o_ref[...] = y


def _ffn(x, g, sh, sc, gt, w1, w2, layer, g_final, final_norm, tm=1024, tf=512):
    bsz, seq, d = x.shape
    f = w1.shape[2]
    tm = min(tm, seq)
    return pl.pallas_call(
        functools.partial(_ffn_kernel, final_norm=final_norm),
        grid=(bsz, seq // tm, f // tf),
        in_specs=[pl.BlockSpec((None, tm, d), lambda b, i, k: (b, i, 0)),
                  pl.BlockSpec((1, d), lambda b, i, k: (0, 0)),
                  pl.BlockSpec((None, 1, d), lambda b, i, k: (b, 0, 0)),
                  pl.BlockSpec((None, 1, d), lambda b, i, k: (b, 0, 0)),
                  pl.BlockSpec((None, 1, d), lambda b, i, k: (b, 0, 0)),
                  pl.BlockSpec((None, d, tf), lambda b, i, k: (layer, 0, k)),
                  pl.BlockSpec((None, tf, d), lambda b, i, k: (layer, k, 0)),
                  pl.BlockSpec((1, d), lambda b, i, k: (0, 0))],
        out_specs=pl.BlockSpec((None, tm, d), lambda b, i, k: (b, i, 0)),
        out_shape=jax.ShapeDtypeStruct((bsz, seq, d), F32),
        scratch_shapes=[pltpu.VMEM((tm, d), BF16)],
        compiler_params=_cparams(3),
        name="ffn",
    )(x, g, sh, sc, gt, w1, w2, g_final)


def _scan_constants():
    c = SCAN_C
    t = np.arange(c)[:, None]
    r = np.arange(c)[None, :]
    sums, masks = [], []
    for reverse in (False, True):
        tri = (r >= t) if reverse else (r <= t)
        sums.append(np.concatenate([tri, tri, tri], axis=1))
        m = [(r == t)]
        for lvl in range(SCAN_LEVELS):
            h = 1 << lvl
            same = (t // (2 * h)) == (r // (2 * h))
            if reverse:
                m.append(same & (t % (2 * h) < h) & (r % (2 * h) >= h))
            else:
                m.append(same & (t % (2 * h) >= h) & (r % (2 * h) < h))
        masks.append(np.stack(m))
    return (jnp.asarray(np.stack(sums).astype(np.float32), BF16),
            jnp.asarray(np.stack(masks).astype(np.float32), F32))


def _block_row(x, size, idx):
    n, w = x.shape
    x3 = x.reshape(n // size, size, w)
    return jnp.broadcast_to(x3[:, idx:idx + 1, :], x3.shape).reshape(n, w)


def _scan_prepare(q_ref, f_ref, v_ref, lb_ref, sums_ref, sT_ref, reverse):
    c = SCAN_C
    tt = q_ref.shape[0]
    n_chunks = tt // c
    chunks = [slice(i * c, (i + 1) * c) for i in range(n_chunks)]

    q = q_ref[...]
    f = f_ref[...]
    v = v_ref[...].astype(BF16)
    lb = lb_ref[...]

    e = jnp.exp(-jnp.abs(f))
    r = 1.0 / (1.0 + e)
    er = e * r
    pos = f >= 0.0
    fg = jnp.maximum(lb + (1.0 - lb) * jnp.where(pos, r, er), LOG_FLOOR)
    g = jnp.log(fg)
    k = (1.0 - lb) * jnp.where(pos, er, r)

    hi, mid, lo = _split3(g)
    sums = sums_ref[...]
    b2 = LOG2E * jnp.concatenate(
        [_dot(sums, jnp.concatenate([hi[s], mid[s], lo[s]], axis=0)) for s in chunks], axis=0)

    b2_end = _block_row(b2, c, 0 if reverse else c - 1)
    qe = (q * jnp.exp2(b2)).astype(BF16)
    kd = (k * jnp.exp2(b2_end - b2)).astype(BF16)
    dec = jnp.exp2(b2_end)

    sT = sT_ref[...]
    o_inter = [None] * n_chunks
    for ci in (range(n_chunks - 1, -1, -1) if reverse else range(n_chunks)):
        s = chunks[ci]
        o_inter[ci] = _dot_nt(qe[s], sT.astype(BF16))
        sT = sT * dec[ci * c:ci * c + 1] + _dot_tn(v[s], kd[s])
    sT_ref[...] = sT
    return dict(chunks=chunks, q=q.astype(BF16), k=k.astype(BF16), v=v, fg=fg, b2=b2, o_inter=o_inter)


def _scan_scores(st, masks_ref, reverse):
    q, k, fg, b2, chunks = st["q"], st["k"], st["fg"], st["b2"], st["chunks"]
    tt = q.shape[0]

    def level(a, lvl, ql, kl):
        return [(0.0 if a is None else a[i]) + masks_ref[lvl] * _dot_nt(ql[s], kl[s])
                for i, s in enumerate(chunks)]

    a = level(None, 0, q, k)
    a = level(a, 1, q * fg.astype(BF16), k)
    pos4 = lax.broadcasted_iota(jnp.int32, fg.shape, 0) & 3
    fg_next = pltpu.roll(fg, tt - 1, 0)
    fg_prev = pltpu.roll(fg, 1, 0)
    if reverse:
        w4 = jnp.where(pos4 == 0, fg * fg_next, jnp.where(pos4 == 1, fg, jnp.where(pos4 == 2, 1.0, fg_prev)))
    else:
        w4 = jnp.where(pos4 == 0, fg_next, jnp.where(pos4 == 1, 1.0, jnp.where(pos4 == 2, fg, fg * fg_prev)))
    w4 = w4.astype(BF16)
    a = level(a, 2, q * w4, k * w4)
    for lvl in range(2, SCAN_LEVELS):
        h = 1 << lvl
        w = jnp.exp2(-jnp.abs(b2 - _block_row(b2, 2 * h, h if reverse else h - 1))).astype(BF16)
        a = level(a, lvl + 1, q * w, k * w)
    return a


def _scan_finish(st, a, o_ref):
    for i, s in enumerate(st["chunks"]):
        o_ref[s, :] = st["o_inter"][i] + _dot(a[i].astype(BF16), st["v"][s])


def _scan_kernel(qf_ref, ff_ref, vf_ref, qb_ref, fb_ref, vb_ref, lbf_ref, lbb_ref, s0f_ref, s0b_ref,
                 sums_ref, masks_ref, of_ref, ob_ref, sTf_ref, sTb_ref):
    @pl.when(pl.program_id(2) == 0)
    def _():
        sTf_ref[...] = s0f_ref[...]
        sTb_ref[...] = s0b_ref[...]

    st_f = _scan_prepare(qf_ref, ff_ref, vf_ref, lbf_ref, sums_ref.at[0], sTf_ref, False)
    st_b = _scan_prepare(qb_ref, fb_ref, vb_ref, lbb_ref, sums_ref.at[1], sTb_ref, True)
    a_f = _scan_scores(st_f, masks_ref.at[0], False)
    a_b = _scan_scores(st_b, masks_ref.at[1], True)
    _scan_finish(st_f, a_f, of_ref)
    _scan_finish(st_b, a_b, ob_ref)


def _hgrn_scan(p, lb_f, lb_b, s0_f, s0_b, tt=512):
    bsz, seq, _ = p.shape
    tt = min(tt, seq)
    nt = seq // tt
    sums, masks = _scan_constants()
    hd = HEAD_DIM

    def tok(blk, rev):
        if rev:
            return pl.BlockSpec((None, tt, hd), lambda b, h, t: (b, nt - 1 - t, blk + h))
        return pl.BlockSpec((None, tt, hd), lambda b, h, t: (b, t, blk + h))

    lb_spec = pl.BlockSpec((1, hd), lambda b, h, t: (0, h))
    st_spec = pl.BlockSpec((None, None, hd, hd), lambda b, h, t: (b, h, 0, 0))
    o_shape = jax.ShapeDtypeStruct((bsz, seq, A_WIDTH), F32)
    st_shape = jax.ShapeDtypeStruct((bsz, A_HEADS, hd, hd), F32)
    return pl.pallas_call(
        _scan_kernel,
        grid=(bsz, A_HEADS, nt),
        in_specs=[tok(Q_BLK, False), tok(FF_BLK, False), tok(I_BLK, False),
                  tok(Q_BLK, True), tok(FB_BLK, True), tok(I_BLK, True),
                  lb_spec, lb_spec, st_spec, st_spec,
                  pl.BlockSpec(sums.shape, lambda b, h, t: (0, 0, 0)),
                  pl.BlockSpec(masks.shape, lambda b, h, t: (0, 0, 0, 0))],
        out_specs=[tok(0, False), tok(0, True), st_spec, st_spec],
        out_shape=[o_shape, o_shape, st_shape, st_shape],
        compiler_params=_cparams(3),
        name="hgrn_scan",
    )(p, p, p, p, p, p, lb_f, lb_b, s0_f, s0_b, sums, masks)


def _gelu(x):
    return 0.5 * x * (1.0 + lax.erf(x * np.float32(1.0 / np.sqrt(2.0))))


def _mix_out_kernel(of_ref, ob_ref, g_ref, u_ref, v_ref, x_ref, gt_ref, an_ref, vn_ref, ws_ref, bs_ref,
                    wo_ref, o_ref, y_ref):
    tm = x_ref.shape[0]
    hd = HEAD_DIM
    o = of_ref[...] + ob_ref[...]
    gate = g_ref[...]
    gate = gate * (1.0 / (1.0 + jnp.exp(-gate)))
    an = an_ref[...]
    for h in range(A_HEADS):
        cols = slice(h * hd, (h + 1) * hd)
        oh = o[:, cols]
        ms = jnp.mean(oh * oh, axis=-1, keepdims=True)
        y_ref[:, cols] = (oh * lax.rsqrt(ms + EPS) * an[:, cols] * gate[:, cols]).astype(BF16)

    u = _gelu(u_ref[...])
    vg = _gelu(v_ref[...])
    vng = vn_ref[...]
    for gi in range(B_GROUPS):
        cols = slice(gi * hd, (gi + 1) * hd)
        vh = vg[:, cols]
        mu = jnp.mean(vh, axis=-1, keepdims=True)
        dv = vh - mu
        var = jnp.mean(dv * dv, axis=-1, keepdims=True)
        vn = (dv * lax.rsqrt(var + EPS) * vng[:, cols]).astype(BF16)
        ws = ws_ref[gi]
        bias = bs_ref[gi]
        for n in range(tm // B_CHUNK):
            rows = slice(n * B_CHUNK, (n + 1) * B_CHUNK)
            mixed = _dot(ws, vn[rows]) + bias
            y_ref[rows, A_WIDTH + gi * hd:A_WIDTH + (gi + 1) * hd] = (u[rows, cols] * mixed).astype(BF16)

    o_ref[...] = x_ref[...] + gt_ref[...] * _dot(y_ref[...], wo_ref[...])


def _mix_out(o_f, o_b, p, x, gt, a_norm_g, b_vnorm_g, ws, bias, w_out, layer, tm=256):
    bsz, seq, d = x.shape
    tm = min(tm, seq)
    wd = A_WIDTH
    return pl.pallas_call(
        _mix_out_kernel,
        grid=(bsz, seq // tm),
        in_specs=[pl.BlockSpec((None, tm, wd), lambda b, i: (b, i, 0)),
                  pl.BlockSpec((None, tm, wd), lambda b, i: (b, i, 0)),
                  pl.BlockSpec((None, tm, wd), lambda b, i: (b, i, G_BLK)),
                  pl.BlockSpec((None, tm, wd), lambda b, i: (b, i, U_BLK)),
                  pl.BlockSpec((None, tm, wd), lambda b, i: (b, i, V_BLK)),
                  pl.BlockSpec((None, tm, d), lambda b, i: (b, i, 0)),
                  pl.BlockSpec((None, 1, d), lambda b, i: (b, 0, 0)),
                  pl.BlockSpec((1, wd), lambda b, i: (0, 0)),
                  pl.BlockSpec((1, wd), lambda b, i: (0, 0)),
                  pl.BlockSpec((None,) + ws.shape[1:], lambda b, i: (layer, 0, 0, 0)),
                  pl.BlockSpec((None,) + bias.shape[1:], lambda b, i: (layer, 0, 0, 0)),
                  pl.BlockSpec((None,) + w_out.shape[1:], lambda b, i: (layer, 0, 0))],
        out_specs=pl.BlockSpec((None, tm, d), lambda b, i: (b, i, 0)),
        out_shape=jax.ShapeDtypeStruct((bsz, seq, d), F32),
        scratch_shapes=[pltpu.VMEM((tm, A_WIDTH + B_WIDTH), BF16)],
        compiler_params=_cparams(2),
        name="mix_out",
    )(o_f, o_b, p, p, p, x, gt, a_norm_g, b_vnorm_g, ws, bias, w_out)


def _pool_constants(width):
    t = np.arange(POOL_MM)[:, None]
    s = np.arange(POOL_MM)[None, :]
    mats = []
    for k in POOL_WINDOWS:
        same_row = (t // width) == (s // width)
        lo = (t % width) - k // 2
        m = same_row & ((s % width) >= lo) & ((s % width) < lo + k)
        mats.append(np.concatenate([m, m, m], axis=1))
    return jnp.asarray(np.stack(mats).astype(np.float32), BF16)


def _pool_group(k, width, n_rows, has_halo, p_ref, prev_ref, next_ref, x_ref, gt_ref, wg_ref, bg_ref, sp_ref,
                cm_ref, o_ref):
    tp = p_ref.shape[0]
    i = pl.program_id(1)
    cur = p_ref[...]
    if has_halo:
        half = (k // 2) * width
        zeros = jnp.zeros((half, cur.shape[1]), F32)
        top = jnp.where(i > 0, prev_ref[POOL_HALO - half:, :], zeros)
        bot = jnp.where(i < pl.num_programs(1) - 1, next_ref[:half, :], zeros)
        acc = jnp.concatenate([top, cur, bot], axis=0)
        step = width
        while step < k * width:
            acc = acc[:acc.shape[0] - step] + acc[step:]
            step *= 2
        rs = acc[:tp]
    else:
        rs = cur

    tok = lax.broadcasted_iota(jnp.int32, (tp, 1), 0)
    col = tok & (width - 1)
    row = (tok >> (width.bit_length() - 1)) + i * (tp // width)
    ccnt = jnp.minimum(col - k // 2 + k, width) - jnp.maximum(col - k // 2, 0)
    rcnt = jnp.minimum(row - k // 2 + k, n_rows) - jnp.maximum(row - k // 2, 0)
    inv = 1.0 / (ccnt * rcnt).astype(F32)

    cm = cm_ref[...]
    parts = []
    for n in range(tp // POOL_MM):
        blk = rs[n * POOL_MM:(n + 1) * POOL_MM]
        parts.append(_dot(cm, jnp.concatenate(_split3(blk), axis=0)))
    pooled = jnp.concatenate(parts, axis=0) if len(parts) > 1 else parts[0]
    z = pooled * inv - cur
    y = (_dot(z.astype(BF16), wg_ref[...]) + bg_ref[...]) * sp_ref[...]
    o_ref[...] = x_ref[...] + gt_ref[...] * y


def _pool_kernel(*refs, width, n_rows, has_halo):
    if has_halo:
        p_ref, prev_ref, next_ref = refs[:3]
        rest = refs[3:]
    else:
        p_ref, prev_ref, next_ref = refs[0], None, None
        rest = refs[1:]
    gi = pl.program_id(2)
    for idx, k in enumerate(POOL_WINDOWS):
        @pl.when(gi == idx)
        def _(k=k):
            _pool_group(k, width, n_rows, has_halo, p_ref, prev_ref, next_ref, *rest)


def _pool_mix(p, x, gt, w_grp, b_grp, scale, layer, on_grid, tp=1024):
    bsz, seq, d = x.shape
    cc = d // C_GROUPS
    if on_grid:
        width, n_rows = GRID_W, seq // GRID_W
        tp = min(tp, seq)
    else:
        width, n_rows = seq, 1
        tp = seq
    has_halo = n_rows > 1
    assert width & (width - 1) == 0 and POOL_MM % width == 0 and tp % POOL_MM == 0
    cm = _pool_constants(width)
    nt = seq // tp
    in_specs = [pl.BlockSpec((None, tp, cc), lambda b, i, g: (b, i, g))]
    args = [p]
    if has_halo:
        assert tp % POOL_HALO == 0 and max(POOL_WINDOWS) // 2 * width <= POOL_HALO
        ratio = tp // POOL_HALO
        last = seq // POOL_HALO - 1
        in_specs += [
            pl.BlockSpec((None, POOL_HALO, cc), lambda b, i, g: (b, jnp.maximum(i * ratio - 1, 0), g)),
            pl.BlockSpec((None, POOL_HALO, cc), lambda b, i, g: (b, jnp.minimum((i + 1) * ratio, last), g))]
        args += [p, p]
    in_specs += [pl.BlockSpec((None, tp, cc), lambda b, i, g: (b, i, g)),
                 pl.BlockSpec((None, 1, cc), lambda b, i, g: (b, 0, g)),
                 pl.BlockSpec((None, None, cc, cc), lambda b, i, g: (layer, g, 0, 0)),
                 pl.BlockSpec((None, None, 1, cc), lambda b, i, g: (layer, g, 0, 0)),
                 pl.BlockSpec((1, cc), lambda b, i, g: (0, g)),
                 pl.BlockSpec((None,) + cm.shape[1:], lambda b, i, g: (g, 0, 0))]
    args += [x, gt, w_grp, b_grp, scale, cm]
    return pl.pallas_call(
        functools.partial(_pool_kernel, width=width, n_rows=n_rows, has_halo=has_halo),
        grid=(bsz, nt, C_GROUPS),
        in_specs=in_specs,
        out_specs=pl.BlockSpec((None, tp, cc), lambda b, i, g: (b, i, g)),
        out_shape=jax.ShapeDtypeStruct((bsz, seq, d), F32),
        compiler_params=_cparams(3),
        name="pool_mix_grid" if on_grid else "pool_mix_seq",
    )(*args)


def _lower_bounds(lb_logits):
    p = jax.nn.softmax(lb_logits.astype(F32), axis=1)
    return jnp.cumsum(p, axis=1) - p[:, :1]


def kernel(x, c, ctx, c_ctx, w_ada, b_ada, g_norm_mix, g_norm_ffn, w_in_even, w_out_even, lb_logits,
           g_hgrn_out, w_spatial, b_spatial, g_spatial_v, w_in_pool, w_grp_pool, b_grp_pool, scale_pool,
           w_ffn_up, w_ffn_down, g_norm_final):
    bsz, seq, d = x.shape
    depth = w_ada.shape[0]
    last_even = depth - 1 if (depth - 1) % 2 == 0 else depth - 2
    assert bsz + 1 <= 8

    ct = jnp.zeros((d, 8), F32).at[:, :bsz].set(c.T).at[:, bsz].set(c_ctx)
    mods = _ada_mods(ct, w_ada, b_ada).reshape(depth, 8, N_MOD, d)

    def lat_mod(layer, m):
        return mods[layer, :bsz, m][:, None, :]

    def ctx_mod(layer, m):
        return jnp.broadcast_to(mods[layer, bsz, m][None, None, :], (bsz, 1, d))

    lbs = _lower_bounds(lb_logits)
    w_in_even_b = w_in_even.astype(BF16)
    w_out_even_b = w_out_even.astype(BF16)
    w_spatial_b = w_spatial.astype(BF16)
    w_in_pool_b = w_in_pool.astype(BF16)
    w_grp_pool_b = w_grp_pool.astype(BF16)
    w_up_b = w_ffn_up.astype(BF16)
    w_down_b = w_ffn_down.astype(BF16)
    b_spatial_col = b_spatial[:, :, :, None]
    b_grp_row = b_grp_pool[:, :, None, :]
    g_final = g_norm_final.reshape(1, d)
    zero_state = jnp.zeros((bsz, A_HEADS, HEAD_DIM, HEAD_DIM), F32)

    def even_mixer(h_in, mod, layer, s0_f, s0_b, need_out):
        e = layer // 2
        g_mix = g_norm_mix[layer].reshape(1, d)
        p = _norm_mod_proj(h_in, g_mix, mod(layer, 0), mod(layer, 1), w_in_even_b, e)
        o_f, o_b, s_f, s_b = _hgrn_scan(p, lbs[0, e].reshape(1, -1), lbs[1, e].reshape(1, -1), s0_f, s0_b)
        if not need_out:
            return None, s_f, s_b
        y = _mix_out(o_f, o_b, p, h_in, mod(layer, 2), g_hgrn_out[e].reshape(1, -1),
                     g_spatial_v[e].reshape(1, -1), w_spatial_b, b_spatial_col, w_out_even_b, e)
        return y, s_f, s_b

    def pool_mixer(h_in, mod, layer, on_grid):
        o = layer // 2
        g_mix = g_norm_mix[layer].reshape(1, d)
        p = _norm_mod_proj(h_in, g_mix, mod(layer, 0), mod(layer, 1), w_in_pool_b, o)
        return _pool_mix(p, h_in, mod(layer, 2), w_grp_pool_b, b_grp_row, scale_pool[o].reshape(1, d), o, on_grid)

    def ffn(h_in, mod, layer, final_norm):
        return _ffn(h_in, g_norm_ffn[layer].reshape(1, d), mod(layer, 3), mod(layer, 4), mod(layer, 5),
                    w_up_b, w_down_b, layer, g_final, final_norm)

    ctx_s = ctx
    for layer in range(depth):
        ctx_out_needed = layer < last_even
        ctx_read = layer <= last_even
        if layer % 2 == 0:
            if ctx_read:
                yc, s_f, s_b = even_mixer(ctx_s, ctx_mod, layer, zero_state, zero_state, ctx_out_needed)
            else:
                yc, s_f, s_b = None, zero_state, zero_state
            x, _, _ = even_mixer(x, lat_mod, layer, s_f, s_b, True)
        else:
            yc = pool_mixer(ctx_s, ctx_mod, layer, False) if ctx_out_needed else None
            x = pool_mixer(x, lat_mod, layer, True)
        x = ffn(x, lat_mod, layer, layer == depth - 1)
        if ctx_out_needed:
            ctx_s = ffn(yc, ctx_mod, layer, False)
    return x
```

```python
import functools

import numpy as np
import jax
import jax.numpy as jnp
from jax import lax
from jax.experimental import pallas as pl
from jax.experimental.pallas import tpu as pltpu

F32 = jnp.float32
BF16 = jnp.bfloat16

EPS = 1e-6
LOG_FLOOR = 1e-30
N_MOD = 6

HEAD_DIM = 128
A_HEADS = 8
B_GROUPS = 8
A_KWIDTH = A_HEADS * HEAD_DIM
A_WIDTH = A_HEADS * HEAD_DIM
B_WIDTH = B_GROUPS * HEAD_DIM
B_CHUNK = 128
Q_BLK, FF_BLK, FB_BLK, I_BLK = 0, 8, 16, 24
G_BLK, U_BLK, V_BLK = 4, 5, 6
GRID_W = 64
POOL_WINDOWS = (2, 4, 8, 16)
C_GROUPS = 4

SCAN_C = 64
SCAN_LEVELS = 6
POOL_MM = 256
POOL_HALO = 512

VMEM_LIMIT = 56 * 1024 * 1024
LOG2E = float(np.log2(np.e))
LANES = 128
SUBLANES = 8
NORM_ROWS = 16


def _cparams(n_grid):
    return pltpu.CompilerParams(dimension_semantics=("arbitrary",) * n_grid,
                                vmem_limit_bytes=VMEM_LIMIT)


def _dot(a, b):
    return jnp.dot(a, b, preferred_element_type=F32)


def _dot_nt(a, b):
    return lax.dot_general(a, b, (((1,), (1,)), ((), ())), preferred_element_type=F32)


def _dot_tn(a, b):
    return lax.dot_general(a, b, (((0,), (0,)), ((), ())), preferred_element_type=F32)


def _split3(x):
    hi = x.astype(BF16)
    r1 = x - hi.astype(F32)
    mid = r1.astype(BF16)
    lo = (r1 - mid.astype(F32)).astype(BF16)
    return hi, mid, lo


def _norm_mod_rows(x_ref, g_ref, sh_ref, sc_ref, h_ref, rs_ref, gs_ref, shb_ref):
    tm, d = x_ref.shape
    gs_ref[...] = jnp.broadcast_to(g_ref[...] * (1.0 + sc_ref[...]), gs_ref.shape)
    shb_ref[...] = jnp.broadcast_to(sh_ref[...], shb_ref.shape)
    lane_tiles = [slice(j * LANES, (j + 1) * LANES) for j in range(d // LANES)]
    sub = SUBLANES

    def sum_squares(i, carry):
        r = pl.ds(pl.multiple_of(i * NORM_ROWS, NORM_ROWS), NORM_ROWS)
        parts = [None, None]
        for j, cols in enumerate(lane_tiles):
            xv = x_ref[r, cols]
            parts[j % 2] = xv * xv if parts[j % 2] is None else parts[j % 2] + xv * xv
        rs_ref[r, :] = parts[0] + parts[1]
        return carry

    def apply(i, carry):
        r0 = pl.multiple_of(i * NORM_ROWS, NORM_ROWS)
        rs = rs_ref[pl.ds(r0, NORM_ROWS), :]
        for cols in lane_tiles:
            gs = gs_ref[:, cols]
            shb = shb_ref[:, cols]
            halves = [x_ref[pl.ds(r0 + k * sub, sub), cols] * rs[k * sub:(k + 1) * sub] * gs + shb
                      for k in range(NORM_ROWS // sub)]
            h_ref[pl.ds(r0, NORM_ROWS), cols] = jnp.concatenate(halves, axis=0).astype(BF16)
        return carry

    lax.fori_loop(0, tm // NORM_ROWS, sum_squares, 0, unroll=4)
    ms = jnp.sum(rs_ref[...], axis=-1, keepdims=True) * (1.0 / d)
    rs_ref[...] = jnp.broadcast_to(lax.rsqrt(ms + EPS), rs_ref.shape)
    lax.fori_loop(0, tm // NORM_ROWS, apply, 0, unroll=2)


def _norm_scratch(tm, d):
    return [pltpu.VMEM((tm, d), BF16), pltpu.VMEM((tm, LANES), F32),
            pltpu.VMEM((SUBLANES, d), F32), pltpu.VMEM((SUBLANES, d), F32)]


def _ada_kernel(ct_ref, w_ref, b_ref, o_ref):
    ct = ct_ref[...]
    st = ct * (1.0 / (1.0 + jnp.exp(-ct)))
    w = w_ref[...]
    b = b_ref[...]
    rows = [jnp.sum(st[:, r:r + 1] * w, axis=0, keepdims=True) + b for r in range(3)]
    rows.append(jnp.zeros((5, w.shape[1]), F32))
    o_ref[...] = jnp.concatenate(rows, axis=0)


def _ada_mods(ct, w_ada, b_ada, tn=512):
    depth, d, n = w_ada.shape
    return pl.pallas_call(
        _ada_kernel,
        grid=(depth, n // tn),
        in_specs=[pl.BlockSpec((d, 8), lambda l, j: (0, 0)),
                  pl.BlockSpec((None, d, tn), lambda l, j: (l, 0, j)),
                  pl.BlockSpec((None, 1, tn), lambda l, j: (l, 0, j))],
        out_specs=pl.BlockSpec((None, 8, tn), lambda l, j: (l, 0, j)),
        out_shape=jax.ShapeDtypeStruct((depth, 8, n), F32),
        compiler_params=_cparams(2),
        name="ada_mods",
    )(ct, w_ada, b_ada.reshape(depth, 1, n))


def _proj_kernel(x_ref, g_ref, sh_ref, sc_ref, w_ref, o_ref, h_ref, rs_ref, gs_ref, shb_ref):
    @pl.when(pl.program_id(2) == 0)
    def _():
        _norm_mod_rows(x_ref, g_ref, sh_ref, sc_ref, h_ref, rs_ref, gs_ref, shb_ref)

    o_ref[...] = _dot(h_ref[...], w_ref[...])


def _norm_mod_proj(x, g, sh, sc, w, layer, tm=1024, tn=1024):
    bsz, seq, d = x.shape
    n = w.shape[2]
    tm = min(tm, seq)
    tn = min(tn, n)
    return pl.pallas_call(
        _proj_kernel,
        grid=(bsz, seq // tm, n // tn),
        in_specs=[pl.BlockSpec((None, tm, d), lambda b, i, j: (b, i, 0)),
                  pl.BlockSpec((1, d), lambda b, i, j: (0, 0)),
                  pl.BlockSpec((None, 1, d), lambda b, i, j: (b, 0, 0)),
                  pl.BlockSpec((None, 1, d), lambda b, i, j: (b, 0, 0)),
                  pl.BlockSpec((None, d, tn), lambda b, i, j: (layer, 0, j))],
        out_specs=pl.BlockSpec((None, tm, tn), lambda b, i, j: (b, i, j)),
        out_shape=jax.ShapeDtypeStruct((bsz, seq, n), F32),
        scratch_shapes=_norm_scratch(tm, d),
        compiler_params=_cparams(3),
        name="norm_mod_proj",
    )(x, g, sh, sc, w)


def _ffn_kernel(x_ref, g_ref, sh_ref, sc_ref, gt_ref, w1_ref, w2_ref, gf_ref, o_ref, h_ref, rs_ref, gs_ref,
                shb_ref, *, final_norm):
    k = pl.program_id(2)

    @pl.when(k == 0)
    def _():
        _norm_mod_rows(x_ref, g_ref, sh_ref, sc_ref, h_ref, rs_ref, gs_ref, shb_ref)

    u = jnp.maximum(_dot(h_ref[...], w1_ref[...]), 0.0)
    u = (u * u).astype(BF16)

    @pl.when(k == 0)
    def _():
        o_ref[...] = _dot(u, w2_ref[...])

    @pl.when(k > 0)
    def _():
        o_ref[...] += _dot(u, w2_ref[...])

    @pl.when(k == pl.num_programs(2) - 1)
    def _():
        y = x_ref[...] + gt_ref[...] * o_ref[...]
        if final_norm:
            ms = jnp.mean(y * y, axis=-1, keepdims=True)
            y = y * lax.rsqrt(ms + EPS) * gf_ref[...]
        o_ref[...] = y


def _ffn(x, g, sh, sc, gt, w1, w2, layer, g_final, final_norm, tm=1024, tf=512):
    bsz, seq, d = x.shape
    f = w1.shape[2]
    tm = min(tm, seq)
    return pl.pallas_call(
        functools.partial(_ffn_kernel, final_norm=final_norm),
        grid=(bsz, seq // tm, f // tf),
        in_specs=[pl.BlockSpec((None, tm, d), lambda b, i, k: (b, i, 0)),
                  pl.BlockSpec((1, d), lambda b, i, k: (0, 0)),
                  pl.BlockSpec((None, 1, d), lambda b, i, k: (b, 0, 0)),
                  pl.BlockSpec((None, 1, d), lambda b, i, k: (b, 0, 0)),
                  pl.BlockSpec((None, 1, d), lambda b, i, k: (b, 0, 0)),
                  pl.BlockSpec((None, d, tf), lambda b, i, k: (layer, 0, k)),
                  pl.BlockSpec((None, tf, d), lambda b, i, k: (layer, k, 0)),
                  pl.BlockSpec((1, d), lambda b, i, k: (0, 0))],
        out_specs=pl.BlockSpec((None, tm, d), lambda b, i, k: (b, i, 0)),
        out_shape=jax.ShapeDtypeStruct((bsz, seq, d), F32),
        scratch_shapes=_norm_scratch(tm, d),
        compiler_params=_cparams(3),
        name="ffn",
    )(x, g, sh, sc, gt, w1, w2, g_final)


def _scan_constants():
    c = SCAN_C
    t = np.arange(c)[:, None]
    r = np.arange(c)[None, :]
    sums, masks = [], []
    for reverse in (False, True):
        tri = (r >= t) if reverse else (r <= t)
        sums.append(np.concatenate([tri, tri, tri], axis=1))
        m = [(r == t)]
        for lvl in range(SCAN_LEVELS):
            h = 1 << lvl
            same = (t // (2 * h)) == (r // (2 * h))
            if reverse:
                m.append(same & (t % (2 * h) < h) & (r % (2 * h) >= h))
            else:
                m.append(same & (t % (2 * h) >= h) & (r % (2 * h) < h))
        masks.append(np.stack(m))
    return (jnp.asarray(np.stack(sums).astype(np.float32), BF16),
            jnp.asarray(np.stack(masks).astype(np.float32), F32))


def _block_row(x, size, idx):
    n, w = x.shape
    x3 = x.reshape(n // size, size, w)
    return jnp.broadcast_to(x3[:, idx:idx + 1, :], x3.shape).reshape(n, w)


def _scan_prepare(q_ref, f_ref, v_ref, lb_ref, sums_ref, sT_ref, reverse):
    c = SCAN_C
    tt = q_ref.shape[0]
    n_chunks = tt // c
    chunks = [slice(i * c, (i + 1) * c) for i in range(n_chunks)]

    q = q_ref[...]
    f = f_ref[...]
    v = v_ref[...].astype(BF16)
    lb = lb_ref[...]

    e = jnp.exp(-jnp.abs(f))
    r = 1.0 / (1.0 + e)
    er = e * r
    pos = f >= 0.0
    fg = jnp.maximum(lb + (1.0 - lb) * jnp.where(pos, r, er), LOG_FLOOR)
    g = jnp.log(fg)
    k = (1.0 - lb) * jnp.where(pos, er, r)

    hi, mid, lo = _split3(g)
    sums = sums_ref[...]
    b2 = LOG2E * jnp.concatenate(
        [_dot(sums, jnp.concatenate([hi[s], mid[s], lo[s]], axis=0)) for s in chunks], axis=0)

    b2_end = _block_row(b2, c, 0 if reverse else c - 1)
    qe = (q * jnp.exp2(b2)).astype(BF16)
    kd = (k * jnp.exp2(b2_end - b2)).astype(BF16)
    dec = jnp.exp2(b2_end)

    sT = sT_ref[...]
    o_inter = [None] * n_chunks
    for ci in (range(n_chunks - 1, -1, -1) if reverse else range(n_chunks)):
        s = chunks[ci]
        o_inter[ci] = _dot_nt(qe[s], sT.astype(BF16))
        sT = sT * dec[ci * c:ci * c + 1] + _dot_tn(v[s], kd[s])
    sT_ref[...] = sT
    return dict(chunks=chunks, q=q.astype(BF16), k=k.astype(BF16), v=v, fg=fg, b2=b2, o_inter=o_inter)


def _scan_scores(st, masks_ref, reverse):
    q, k, fg, b2, chunks = st["q"], st["k"], st["fg"], st["b2"], st["chunks"]
    tt = q.shape[0]

    def level(a, lvl, ql, kl):
        return [(0.0 if a is None else a[i]) + masks_ref[lvl] * _dot_nt(ql[s], kl[s])
                for i, s in enumerate(chunks)]

    a = level(None, 0, q, k)
    a = level(a, 1, q * fg.astype(BF16), k)
    pos4 = lax.broadcasted_iota(jnp.int32, fg.shape, 0) & 3
    fg_next = pltpu.roll(fg, tt - 1, 0)
    fg_prev = pltpu.roll(fg, 1, 0)
    if reverse:
        w4 = jnp.where(pos4 == 0, fg * fg_next, jnp.where(pos4 == 1, fg, jnp.where(pos4 == 2, 1.0, fg_prev)))
    else:
        w4 = jnp.where(pos4 == 0, fg_next, jnp.where(pos4 == 1, 1.0, jnp.where(pos4 == 2, fg, fg * fg_prev)))
    w4 = w4.astype(BF16)
    a = level(a, 2, q * w4, k * w4)
    for lvl in range(2, SCAN_LEVELS):
        h = 1 << lvl
        w = jnp.exp2(-jnp.abs(b2 - _block_row(b2, 2 * h, h if reverse else h - 1))).astype(BF16)
        a = level(a, lvl + 1, q * w, k * w)
    return a


def _scan_finish(st, a, o_ref):
    for i, s in enumerate(st["chunks"]):
        o_ref[s, :] = st["o_inter"][i] + _dot(a[i].astype(BF16), st["v"][s])


def _scan_kernel(qf_ref, ff_ref, vf_ref, qb_ref, fb_ref, vb_ref, lbf_ref, lbb_ref, s0f_ref, s0b_ref,
                 sums_ref, masks_ref, of_ref, ob_ref, sTf_ref, sTb_ref):
    @pl.when(pl.program_id(2) == 0)
    def _():
        sTf_ref[...] = s0f_ref[...]
        sTb_ref[...] = s0b_ref[...]

    st_f = _scan_prepare(qf_ref, ff_ref, vf_ref, lbf_ref, sums_ref.at[0], sTf_ref, False)
    st_b = _scan_prepare(qb_ref, fb_ref, vb_ref, lbb_ref, sums_ref.at[1], sTb_ref, True)
    a_f = _scan_scores(st_f, masks_ref.at[0], False)
    a_b = _scan_scores(st_b, masks_ref.at[1], True)
    _scan_finish(st_f, a_f, of_ref)
    _scan_finish(st_b, a_b, ob_ref)


def _hgrn_scan(p, lb_f, lb_b, s0_f, s0_b, tt=512):
    bsz, seq, _ = p.shape
    tt = min(tt, seq)
    nt = seq // tt
    sums, masks = _scan_constants()
    hd = HEAD_DIM

    def tok(blk, rev):
        if rev:
            return pl.BlockSpec((None, tt, hd), lambda b, h, t: (b, nt - 1 - t, blk + h))
        return pl.BlockSpec((None, tt, hd), lambda b, h, t: (b, t, blk + h))

    lb_spec = pl.BlockSpec((1, hd), lambda b, h, t: (0, h))
    st_spec = pl.BlockSpec((None, None, hd, hd), lambda b, h, t: (b, h, 0, 0))
    o_shape = jax.ShapeDtypeStruct((bsz, seq, A_WIDTH), F32)
    st_shape = jax.ShapeDtypeStruct((bsz, A_HEADS, hd, hd), F32)
    return pl.pallas_call(
        _scan_kernel,
        grid=(bsz, A_HEADS, nt),
        in_specs=[tok(Q_BLK, False), tok(FF_BLK, False), tok(I_BLK, False),
                  tok(Q_BLK, True), tok(FB_BLK, True), tok(I_BLK, True),
                  lb_spec, lb_spec, st_spec, st_spec,
                  pl.BlockSpec(sums.shape, lambda b, h, t: (0, 0, 0)),
                  pl.BlockSpec(masks.shape, lambda b, h, t: (0, 0, 0, 0))],
        out_specs=[tok(0, False), tok(0, True), st_spec, st_spec],
        out_shape=[o_shape, o_shape, st_shape, st_shape],
        compiler_params=_cparams(3),
        name="hgrn_scan",
    )(p, p, p, p, p, p, lb_f, lb_b, s0_f, s0_b, sums, masks)


def _gelu(x):
    return 0.5 * x * (1.0 + lax.erf(x * np.float32(1.0 / np.sqrt(2.0))))


def _mix_out_kernel(of_ref, ob_ref, g_ref, u_ref, v_ref, x_ref, gt_ref, an_ref, vn_ref, ws_ref, bs_ref,
                    wo_ref, o_ref, y_ref):
    tm = x_ref.shape[0]
    hd = HEAD_DIM
    an = an_ref[...]
    vng = vn_ref[...]
    gt = gt_ref[...]
    for n in range(tm // B_CHUNK):
        rows = slice(n * B_CHUNK, (n + 1) * B_CHUNK)
        for h in range(A_HEADS):
            cols = slice(h * hd, (h + 1) * hd)
            oh = of_ref[rows, cols] + ob_ref[rows, cols]
            gate = g_ref[rows, cols]
            gate = gate * (1.0 / (1.0 + jnp.exp(-gate)))
            ms = jnp.mean(oh * oh, axis=-1, keepdims=True)
            y_ref[rows, cols] = (oh * lax.rsqrt(ms + EPS) * an[:, cols] * gate).astype(BF16)
        for gi in range(B_GROUPS):
            cols = slice(gi * hd, (gi + 1) * hd)
            vh = _gelu(v_ref[rows, cols])
            mu = jnp.mean(vh, axis=-1, keepdims=True)
            dv = vh - mu
            var = jnp.mean(dv * dv, axis=-1, keepdims=True)
            vn = (dv * lax.rsqrt(var + EPS) * vng[:, cols]).astype(BF16)
            mixed = _dot(ws_ref[gi], vn) + bs_ref[gi]
            y_ref[rows, A_WIDTH + gi * hd:A_WIDTH + (gi + 1) * hd] = (
                _gelu(u_ref[rows, cols]) * mixed).astype(BF16)
        o_ref[rows, :] = x_ref[rows, :] + gt * _dot(y_ref[rows, :], wo_ref[...])


def _mix_out(o_f, o_b, p, x, gt, a_norm_g, b_vnorm_g, ws, bias, w_out, layer, tm=256):
    bsz, seq, d = x.shape
    tm = min(tm, seq)
    wd = A_WIDTH
    return pl.pallas_call(
        _mix_out_kernel,
        grid=(bsz, seq // tm),
        in_specs=[pl.BlockSpec((None, tm, wd), lambda b, i: (b, i, 0)),
                  pl.BlockSpec((None, tm, wd), lambda b, i: (b, i, 0)),
                  pl.BlockSpec((None, tm, wd), lambda b, i: (b, i, G_BLK)),
                  pl.BlockSpec((None, tm, wd), lambda b, i: (b, i, U_BLK)),
                  pl.BlockSpec((None, tm, wd), lambda b, i: (b, i, V_BLK)),
                  pl.BlockSpec((None, tm, d), lambda b, i: (b, i, 0)),
                  pl.BlockSpec((None, 1, d), lambda b, i: (b, 0, 0)),
                  pl.BlockSpec((1, wd), lambda b, i: (0, 0)),
                  pl.BlockSpec((1, wd), lambda b, i: (0, 0)),
                  pl.BlockSpec((None,) + ws.shape[1:], lambda b, i: (layer, 0, 0, 0)),
                  pl.BlockSpec((None,) + bias.shape[1:], lambda b, i: (layer, 0, 0, 0)),
                  pl.BlockSpec((None,) + w_out.shape[1:], lambda b, i: (layer, 0, 0))],
        out_specs=pl.BlockSpec((None, tm, d), lambda b, i: (b, i, 0)),
        out_shape=jax.ShapeDtypeStruct((bsz, seq, d), F32),
        scratch_shapes=[pltpu.VMEM((tm, A_WIDTH + B_WIDTH), BF16)],
        compiler_params=_cparams(2),
        name="mix_out",
    )(o_f, o_b, p, p, p, x, gt, a_norm_g, b_vnorm_g, ws, bias, w_out)


def _pool_constants(width):
    t = np.arange(POOL_MM)[:, None]
    s = np.arange(POOL_MM)[None, :]
    mats = []
    for k in POOL_WINDOWS:
        same_row = (t // width) == (s // width)
        lo = (t % width) - k // 2
        m = same_row & ((s % width) >= lo) & ((s % width) < lo + k)
        mats.append(np.concatenate([m, m, m], axis=1))
    return jnp.asarray(np.stack(mats).astype(np.float32), BF16)


def _pool_group(k, width, n_rows, has_halo, p_ref, prev_ref, next_ref, x_ref, gt_ref, wg_ref, bg_ref, sp_ref,
                cm_ref, o_ref):
    tp = p_ref.shape[0]
    i = pl.program_id(1)
    cur = p_ref[...]
    if has_halo:
        half = (k // 2) * width
        zeros = jnp.zeros((half, cur.shape[1]), F32)
        top = jnp.where(i > 0, prev_ref[POOL_HALO - half:, :], zeros)
        bot = jnp.where(i < pl.num_programs(1) - 1, next_ref[:half, :], zeros)
        acc = jnp.concatenate([top, cur, bot], axis=0)
        step = width
        while step < k * width:
            acc = acc[:acc.shape[0] - step] + acc[step:]
            step *= 2
        rs = acc[:tp]
    else:
        rs = cur

    tok = lax.broadcasted_iota(jnp.int32, (tp, 1), 0)
    col = tok & (width - 1)
    row = (tok >> (width.bit_length() - 1)) + i * (tp // width)
    ccnt = jnp.minimum(col - k // 2 + k, width) - jnp.maximum(col - k // 2, 0)
    rcnt = jnp.minimum(row - k // 2 + k, n_rows) - jnp.maximum(row - k // 2, 0)
    inv = 1.0 / (ccnt * rcnt).astype(F32)

    cm = cm_ref[...]
    parts = []
    for n in range(tp // POOL_MM):
        blk = rs[n * POOL_MM:(n + 1) * POOL_MM]
        parts.append(_dot(cm, jnp.concatenate(_split3(blk), axis=0)))
    pooled = jnp.concatenate(parts, axis=0) if len(parts) > 1 else parts[0]
    z = pooled * inv - cur
    y = (_dot(z.astype(BF16), wg_ref[...]) + bg_ref[...]) * sp_ref[...]
    o_ref[...] = x_ref[...] + gt_ref[...] * y


def _pool_kernel(*refs, width, n_rows, has_halo):
    if has_halo:
        p_ref, prev_ref, next_ref = refs[:3]
        rest = refs[3:]
    else:
        p_ref, prev_ref, next_ref = refs[0], None, None
        rest = refs[1:]
    gi = pl.program_id(2)
    for idx, k in enumerate(POOL_WINDOWS):
        @pl.when(gi == idx)
        def _(k=k):
            _pool_group(k, width, n_rows, has_halo, p_ref, prev_ref, next_ref, *rest)


def _pool_mix(p, x, gt, w_grp, b_grp, scale, layer, on_grid, tp=1024):
    bsz, seq, d = x.shape
    cc = d // C_GROUPS
    if on_grid:
        width, n_rows = GRID_W, seq // GRID_W
        tp = min(tp, seq)
    else:
        width, n_rows = seq, 1
        tp = seq
    has_halo = n_rows > 1
    assert width & (width - 1) == 0 and POOL_MM % width == 0 and tp % POOL_MM == 0
    cm = _pool_constants(width)
    nt = seq // tp
    in_specs = [pl.BlockSpec((None, tp, cc), lambda b, i, g: (b, i, g))]
    args = [p]
    if has_halo:
        assert tp % POOL_HALO == 0 and max(POOL_WINDOWS) // 2 * width <= POOL_HALO
        ratio = tp // POOL_HALO
        last = seq // POOL_HALO - 1
        in_specs += [
            pl.BlockSpec((None, POOL_HALO, cc), lambda b, i, g: (b, jnp.maximum(i * ratio - 1, 0), g)),
            pl.BlockSpec((None, POOL_HALO, cc), lambda b, i, g: (b, jnp.minimum((i + 1) * ratio, last), g))]
        args += [p, p]
    in_specs += [pl.BlockSpec((None, tp, cc), lambda b, i, g: (b, i, g)),
                 pl.BlockSpec((None, 1, cc), lambda b, i, g: (b, 0, g)),
                 pl.BlockSpec((None, None, cc, cc), lambda b, i, g: (layer, g, 0, 0)),
                 pl.BlockSpec((None, None, 1, cc), lambda b, i, g: (layer, g, 0, 0)),
                 pl.BlockSpec((1, cc), lambda b, i, g: (0, g)),
                 pl.BlockSpec((None,) + cm.shape[1:], lambda b, i, g: (g, 0, 0))]
    args += [x, gt, w_grp, b_grp, scale, cm]
    return pl.pallas_call(
        functools.partial(_pool_kernel, width=width, n_rows=n_rows, has_halo=has_halo),
        grid=(bsz, nt, C_GROUPS),
        in_specs=in_specs,
        out_specs=pl.BlockSpec((None, tp, cc), lambda b, i, g: (b, i, g)),
        out_shape=jax.ShapeDtypeStruct((bsz, seq, d), F32),
        compiler_params=_cparams(3),
        name="pool_mix_grid" if on_grid else "pool_mix_seq",
    )(*args)


def _lower_bounds(lb_logits):
    p = jax.nn.softmax(lb_logits.astype(F32), axis=1)
    return jnp.cumsum(p, axis=1) - p[:, :1]


def kernel(x, c, ctx, c_ctx, w_ada, b_ada, g_norm_mix, g_norm_ffn, w_in_even, w_out_even, lb_logits,
           g_hgrn_out, w_spatial, b_spatial, g_spatial_v, w_in_pool, w_grp_pool, b_grp_pool, scale_pool,
           w_ffn_up, w_ffn_down, g_norm_final):
    bsz, seq, d = x.shape
    depth = w_ada.shape[0]
    last_even = depth - 1 if (depth - 1) % 2 == 0 else depth - 2
    assert bsz + 1 <= 8

    ct = jnp.zeros((d, 8), F32).at[:, :bsz].set(c.T).at[:, bsz].set(c_ctx)
    mods = _ada_mods(ct, w_ada, b_ada).reshape(depth, 8, N_MOD, d)

    def lat_mod(layer, m):
        return mods[layer, :bsz, m][:, None, :]

    def ctx_mod(layer, m):
        return jnp.broadcast_to(mods[layer, bsz, m][None, None, :], (bsz, 1, d))

    lbs = _lower_bounds(lb_logits)
    w_in_even_b = w_in_even.astype(BF16)
    w_out_even_b = w_out_even.astype(BF16)
    w_spatial_b = w_spatial.astype(BF16)
    w_in_pool_b = w_in_pool.astype(BF16)
    w_grp_pool_b = w_grp_pool.astype(BF16)
    w_up_b = w_ffn_up.astype(BF16)
    w_down_b = w_ffn_down.astype(BF16)
    b_spatial_col = b_spatial[:, :, :, None]
    b_grp_row = b_grp_pool[:, :, None, :]
    g_final = g_norm_final.reshape(1, d)
    zero_state = jnp.zeros((bsz, A_HEADS, HEAD_DIM, HEAD_DIM), F32)

    def even_mixer(h_in, mod, layer, s0_f, s0_b, need_out):
        e = layer // 2
        g_mix = g_norm_mix[layer].reshape(1, d)
        p = _norm_mod_proj(h_in, g_mix, mod(layer, 0), mod(layer, 1), w_in_even_b, e)
        o_f, o_b, s_f, s_b = _hgrn_scan(p, lbs[0, e].reshape(1, -1), lbs[1, e].reshape(1, -1), s0_f, s0_b)
        if not need_out:
            return None, s_f, s_b
        y = _mix_out(o_f, o_b, p, h_in, mod(layer, 2), g_hgrn_out[e].reshape(1, -1),
                     g_spatial_v[e].reshape(1, -1), w_spatial_b, b_spatial_col, w_out_even_b, e)
        return y, s_f, s_b

    def pool_mixer(h_in, mod, layer, on_grid):
        o = layer // 2
        g_mix = g_norm_mix[layer].reshape(1, d)
        p = _norm_mod_proj(h_in, g_mix, mod(layer, 0), mod(layer, 1), w_in_pool_b, o)
        return _pool_mix(p, h_in, mod(layer, 2), w_grp_pool_b, b_grp_row, scale_pool[o].reshape(1, d), o, on_grid)

    def ffn(h_in, mod, layer, final_norm):
        return _ffn(h_in, g_norm_ffn[layer].reshape(1, d), mod(layer, 3), mod(layer, 4), mod(layer, 5),
                    w_up_b, w_down_b, layer, g_final, final_norm)

    ctx_s = ctx
    for layer in range(depth):
        ctx_out_needed = layer < last_even
        ctx_read = layer <= last_even
        if layer % 2 == 0:
            if ctx_read:
                yc, s_f, s_b = even_mixer(ctx_s, ctx_mod, layer, zero_state, zero_state, ctx_out_needed)
            else:
                yc, s_f, s_b = None, zero_state, zero_state
            x, _, _ = even_mixer(x, lat_mod, layer, s_f, s_b, True)
        else:
            yc = pool_mixer(ctx_s, ctx_mod, layer, False) if ctx_out_needed else None
            x = pool_mixer(x, lat_mod, layer, True)
        x = ffn(x, lat_mod, layer, layer == depth - 1)
        if ctx_out_needed:
            ctx_s = ffn(yc, ctx_mod, layer, False)
    return x
```
